```python
import math
import jax, jax.numpy as jnp
from jax import lax
import numpy as np

D_MODEL = 4096
BATCH = 2
SEQ = 4096
DEPTH = 1

CHUNK = 64
HEAD_DIM = 128
MIX_WIDTH = D_MODEL
FOX_HEADS = MIX_WIDTH // (2 * HEAD_DIM)
GDN_HEADS = MIX_WIDTH // (2 * HEAD_DIM)
FOX_WIDTH = FOX_HEADS * HEAD_DIM
GDN_WIDTH = GDN_HEADS * HEAD_DIM
Q_BLOCK = 128
CONV_K = 4
D_FF = ((8 * D_MODEL + 767) // 768) * 256
N_MOD = 6
EPS = 1e-6

OFF_FOX_Q = 0
OFF_FOX_K = OFF_FOX_Q + FOX_WIDTH
OFF_FOX_V = OFF_FOX_K + FOX_WIDTH
OFF_FOX_F = OFF_FOX_V + FOX_WIDTH
OFF_GDN_QKV = OFF_FOX_F + FOX_HEADS
OFF_GDN_A = OFF_GDN_QKV + 3 * GDN_WIDTH
OFF_GDN_B = OFF_GDN_A + GDN_HEADS
OFF_GDN_Z = OFF_GDN_B + GDN_HEADS
IN_COLS = OFF_GDN_Z + GDN_WIDTH

kernel_name = "hybrid_fox_gdn_adaln_block"


def rms_norm(x, gain):
    xf = x.astype(jnp.float32)
    y = xf * lax.rsqrt(jnp.mean(xf * xf, axis=-1, keepdims=True) + EPS)
    return (y * gain.astype(jnp.float32)).astype(x.dtype)


def l2_normalize(x):
    xf = x.astype(jnp.float32)
    return xf * lax.rsqrt(jnp.sum(xf * xf, axis=-1, keepdims=True) + EPS)


def split_heads(t, n_heads):
    return t.reshape(t.shape[0], t.shape[1], n_heads, -1)


def forgetting_attention(q, k, v, f_logit):
    B, S, H, D = q.shape
    nb = S // Q_BLOCK
    cum = jnp.cumsum(jax.nn.log_sigmoid(f_logit.astype(jnp.float32)), axis=1)
    cum = cum.transpose(0, 2, 1)
    qh, kh, vh = (t.transpose(0, 2, 1, 3) for t in (q, k, v))
    q_blocks = qh.reshape(B, H, nb, Q_BLOCK, D).transpose(2, 0, 1, 3, 4)
    c_blocks = cum.reshape(B, H, nb, Q_BLOCK).transpose(2, 0, 1, 3)
    p_blocks = jnp.arange(S, dtype=jnp.int32).reshape(nb, Q_BLOCK)
    k_pos = jnp.arange(S, dtype=jnp.int32)
    scale = D ** -0.5

    def one_block(args):
        qb, cb, pb = args
        s = jnp.einsum('bhqd,bhkd->bhqk', qb, kh, preferred_element_type=jnp.float32) * scale
        s = s + (cb[..., :, None] - cum[:, :, None, :])
        s = jnp.where(k_pos[None, :] <= pb[:, None], s, -jnp.inf)
        p = jax.nn.softmax(s, axis=-1)
        return jnp.einsum('bhqk,bhkd->bhqd', p.astype(vh.dtype), vh)

    out = lax.map(one_block, (q_blocks, c_blocks, p_blocks))
    return out.transpose(1, 0, 3, 2, 4).reshape(B, S, H * D)


def causal_short_conv(x, w):
    S = x.shape[1]
    xp = jnp.pad(x, ((0, 0), (CONV_K - 1, 0), (0, 0)))
    y = xp[:, 0:S] * w[0]
    for i in range(1, CONV_K):
        y = y + xp[:, i:i + S] * w[i]
    return jax.nn.silu(y)


def chunked_gated_delta_rule(q, k, v, g, beta):
    B, S, H, Dk = q.shape
    Dv = v.shape[-1]
    n = S // CHUNK

    def chunks(t):
        t = jnp.moveaxis(t.astype(jnp.float32), 2, 1)
        return t.reshape((B, H, n, CHUNK) + t.shape[3:])

    q = chunks(q) * (Dk ** -0.5)
    k = chunks(k)
    v = chunks(v)
    beta = chunks(beta)
    g = jnp.cumsum(chunks(g), axis=-1)
    idx = jnp.arange(CHUNK)
    lower = idx[:, None] >= idx[None, :]
    strict = idx[:, None] > idx[None, :]
    decay = jnp.exp(jnp.where(lower, g[..., :, None] - g[..., None, :], -jnp.inf))
    k_beta = k * beta[..., None]
    kk = jnp.einsum('bhnid,bhnjd->bhnij', k_beta, k) * decay
    a_mat = jnp.where(strict, kk, 0.0) + jnp.eye(CHUNK, dtype=jnp.float32)

    def solve(rhs):
        return lax.linalg.triangular_solve(a_mat, rhs, left_side=True, lower=True,
                                           unit_diagonal=True)

    u = solve(v * beta[..., None])
    w = solve(k_beta * jnp.exp(g)[..., None])
    intra = jnp.where(lower, jnp.einsum('bhnid,bhnjd->bhnij', q, k) * decay, 0.0)
    q_dec = q * jnp.exp(g)[..., None]
    k_tail = k * jnp.exp(g[..., -1:] - g)[..., None]
    g_tot = jnp.exp(g[..., -1])

    def step(state, xs):
        u_i, w_i, q_i, k_i, a_i, gt_i = xs
        v_new = u_i - jnp.einsum('bhck,bhkv->bhcv', w_i, state)
        o_i = (jnp.einsum('bhck,bhkv->bhcv', q_i, state)
               + jnp.einsum('bhcj,bhjv->bhcv', a_i, v_new))
        state = state * gt_i[..., None, None] + jnp.einsum('bhck,bhcv->bhkv', k_i, v_new)
        return state, o_i

    xs = tuple(jnp.moveaxis(t, 2, 0) for t in (u, w, q_dec, k_tail, intra, g_tot))
    state0 = jnp.zeros((B, H, Dk, Dv), jnp.float32)
    _, o = lax.scan(step, state0, xs)
    return o.transpose(1, 0, 3, 2, 4).reshape(B, S, H, Dv)


def hybrid_mixer(h, w_in, fox_q_norm, fox_k_norm, fox_f_bias,
                 gdn_conv_w, gdn_a_log, gdn_dt_bias, gdn_norm_w):
    B, S, _ = h.shape
    p = h @ w_in
    fq = rms_norm(split_heads(p[..., OFF_FOX_Q:OFF_FOX_K], FOX_HEADS), fox_q_norm)
    fk = rms_norm(split_heads(p[..., OFF_FOX_K:OFF_FOX_V], FOX_HEADS), fox_k_norm)
    fv = split_heads(p[..., OFF_FOX_V:OFF_FOX_F], FOX_HEADS)
    f_logit = p[..., OFF_FOX_F:OFF_GDN_QKV] + fox_f_bias
    y_fox = forgetting_attention(fq, fk, fv, f_logit).astype(h.dtype)
    qkv = causal_short_conv(p[..., OFF_GDN_QKV:OFF_GDN_A], gdn_conv_w)
    gq, gk, gv = (split_heads(t, GDN_HEADS) for t in jnp.split(qkv, 3, axis=-1))
    gq = l2_normalize(gq)
    gk = l2_normalize(gk)
    a = p[..., OFF_GDN_A:OFF_GDN_B].astype(jnp.float32)
    g = -jnp.exp(gdn_a_log.astype(jnp.float32)) * jax.nn.softplus(a + gdn_dt_bias.astype(jnp.float32))
    beta = jax.nn.sigmoid(p[..., OFF_GDN_B:OFF_GDN_Z].astype(jnp.float32))
    o = chunked_gated_delta_rule(gq, gk, gv, g, beta)
    z = split_heads(p[..., OFF_GDN_Z:IN_COLS], GDN_HEADS).astype(jnp.float32)
    y_gdn = (rms_norm(o, gdn_norm_w) * jax.nn.silu(z)).reshape(B, S, GDN_WIDTH).astype(h.dtype)
    return jnp.concatenate([y_fox, y_gdn], axis=-1)


def swiglu(h, w_gate, w_up, w_down):
    return (jax.nn.silu(h @ w_gate) * (h @ w_up)) @ w_down


def setup_inputs(seed: int = 0) -> dict:
    key = jax.random.key(seed)
    ks = jax.random.split(key, 20)
    f32 = jnp.float32

    def normal(k, shape, scale):
        return jax.random.normal(k, shape, f32) * scale

    x = normal(ks[0], (BATCH, SEQ, D_MODEL), 1.0)
    c = normal(ks[1], (BATCH, D_MODEL), 1.0)
    w_ada = normal(ks[2], (DEPTH, D_MODEL, N_MOD * D_MODEL), 0.5 * D_MODEL ** -0.5)
    b_ada = normal(ks[3], (DEPTH, N_MOD * D_MODEL), 0.02)
    norm1_g = 1.0 + normal(ks[4], (DEPTH, D_MODEL), 0.02)
    w_in = normal(ks[5], (DEPTH, D_MODEL, IN_COLS), D_MODEL ** -0.5)
    fox_q_norm = 1.0 + normal(ks[6], (DEPTH, HEAD_DIM), 0.02)
    fox_k_norm = 1.0 + normal(ks[7], (DEPTH, HEAD_DIM), 0.02)
    fox_f_bias = 3.0 + normal(ks[8], (DEPTH, FOX_HEADS), 0.5)
    gdn_conv_w = normal(ks[9], (DEPTH, CONV_K, 3 * GDN_WIDTH), CONV_K ** -0.5)
    gdn_a_log = jnp.log(jax.random.uniform(ks[10], (DEPTH, GDN_HEADS), f32, 1.0, 16.0))
    dt = jnp.exp(jax.random.uniform(ks[11], (DEPTH, GDN_HEADS), f32,
                                    math.log(1e-3), math.log(1e-1)))
    gdn_dt_bias = dt + jnp.log(-jnp.expm1(-dt))
    gdn_norm_w = 1.0 + normal(ks[12], (DEPTH, HEAD_DIM), 0.02)
    w_out = normal(ks[13], (DEPTH, MIX_WIDTH, D_MODEL), MIX_WIDTH ** -0.5)
    norm2_g = 1.0 + normal(ks[14], (DEPTH, D_MODEL), 0.02)
    w_gate = normal(ks[15], (DEPTH, D_MODEL, D_FF), D_MODEL ** -0.5)
    w_up = normal(ks[16], (DEPTH, D_MODEL, D_FF), D_MODEL ** -0.5)
    w_down = normal(ks[17], (DEPTH, D_FF, D_MODEL), D_FF ** -0.5)
    return {"x": x, "c": c, "w_ada": w_ada, "b_ada": b_ada, "norm1_g": norm1_g,
            "w_in": w_in, "fox_q_norm": fox_q_norm, "fox_k_norm": fox_k_norm,
            "fox_f_bias": fox_f_bias, "gdn_conv_w": gdn_conv_w, "gdn_a_log": gdn_a_log,
            "gdn_dt_bias": gdn_dt_bias, "gdn_norm_w": gdn_norm_w, "w_out": w_out,
            "norm2_g": norm2_g, "w_gate": w_gate, "w_up": w_up, "w_down": w_down}


def reference(x, c, w_ada, b_ada, norm1_g, w_in, fox_q_norm, fox_k_norm, fox_f_bias,
              gdn_conv_w, gdn_a_log, gdn_dt_bias, gdn_norm_w, w_out, norm2_g,
              w_gate, w_up, w_down):
    cond = jax.nn.silu(c)
    for l in range(DEPTH):
        mod = cond @ w_ada[l] + b_ada[l]
        sh1, sc1, g1, sh2, sc2, g2 = [m[:, None, :] for m in jnp.split(mod, N_MOD, axis=-1)]
        h = rms_norm(x, norm1_g[l]) * (1 + sc1) + sh1
        mix = hybrid_mixer(h, w_in[l], fox_q_norm[l], fox_k_norm[l], fox_f_bias[l],
                           gdn_conv_w[l], gdn_a_log[l], gdn_dt_bias[l], gdn_norm_w[l])
        x = x + g1 * (mix @ w_out[l])
        h = rms_norm(x, norm2_g[l]) * (1 + sc2) + sh2
        x = x + g2 * swiglu(h, w_gate[l], w_up[l], w_down[l])
    return x
```

```python
import functools
import math

import jax
import jax.numpy as jnp
from jax import lax
from jax.experimental import pallas as pl
from jax.experimental.pallas import tpu as pltpu

F32 = jnp.float32
BF16 = jnp.bfloat16

HEAD_DIM = 128
CHUNK = 64
CONV_K = 4
N_MOD = 6
EPS = 1e-6
LOG2E = 1.4426950408889634
V7X_VMEM_LIMIT_BYTES = 56 * 1024 * 1024
SUBLANES = 8


def _cparams(sem):
    return pltpu.CompilerParams(dimension_semantics=sem, vmem_limit_bytes=V7X_VMEM_LIMIT_BYTES)


def _silu(x):
    return x * jax.nn.sigmoid(x)


def _dot(a, b):
    return jnp.dot(a, b, preferred_element_type=F32)


def _dot_nt(a, b):
    return lax.dot_general(a, b, (((1,), (1,)), ((), ())), preferred_element_type=F32)


def _dot_tn(a, b):
    return lax.dot_general(a, b, (((0,), (0,)), ((), ())), preferred_element_type=F32)


def _dot_ref_chunks(a_at, w_at, k, tk=512):
    acc = None
    for k0 in range(0, k, tk):
        k1 = min(k0 + tk, k)
        part = _dot(a_at(k0, k1), w_at(k0, k1).astype(BF16))
        acc = part if acc is None else acc + part
    return acc


def _shr(x, n):
    shift = n.bit_length() - 1
    assert 1 << shift == n
    return lax.shift_right_logical(x, shift)


def _softplus(x):
    return jnp.maximum(x, 0.0) + jnp.log1p(jnp.exp(-jnp.abs(x)))


def _ones_where(cond):
    return jnp.where(cond, 1.0, 0.0).astype(BF16)


def _split3(x):
    hi = x.astype(BF16)
    r = x - hi.astype(F32)
    mid = r.astype(BF16)
    lo = (r - mid.astype(F32)).astype(BF16)
    return hi, mid, lo


def _dot_ones_rhs(x, ones_rhs):
    hi, mid, lo = _split3(x)
    return _dot(hi, ones_rhs) + _dot(mid, ones_rhs) + _dot(lo, ones_rhs)


def _dot_ones_lhs(ones_lhs, x):
    hi, mid, lo = _split3(x)
    return _dot(ones_lhs, hi) + _dot(ones_lhs, mid) + _dot(ones_lhs, lo)


def _ada_kernel(c_ref, w_ref, b_ref, o_ref):
    cond = _silu(c_ref[...]).astype(BF16)
    k = c_ref.shape[-1]
    acc = _dot_ref_chunks(lambda a, b: cond[:, a:b], lambda a, b: w_ref[0, a:b, :], k)
    o_ref[...] = acc + b_ref[...]


def _ada_mod(c, w_ada, b_ada, *, tn=512):
    bsz, d = c.shape
    n = w_ada.shape[-1]
    c8 = jnp.zeros((SUBLANES, d), F32).at[:bsz].set(c)
    out = pl.pallas_call(
        _ada_kernel,
        grid=(n // tn,),
        in_specs=[
            pl.BlockSpec((SUBLANES, d), lambda j: (0, 0)),
            pl.BlockSpec((1, d, tn), lambda j: (0, 0, j)),
            pl.BlockSpec((1, tn), lambda j: (0, j)),
        ],
        out_specs=pl.BlockSpec((SUBLANES, tn), lambda j: (0, j)),
        out_shape=jax.ShapeDtypeStruct((SUBLANES, n), F32),
        compiler_params=_cparams(("arbitrary",)),
        name="ada_mod",
    )(c8, w_ada, b_ada)
    return out[:bsz]


def _normmod_kernel(x_ref, g_ref, sc_ref, sh_ref, o_ref):
    x = x_ref[0]
    ms = jnp.mean(x * x, axis=-1, keepdims=True)
    y = x * lax.rsqrt(ms + EPS) * g_ref[...]
    o_ref[0] = (y * (1.0 + sc_ref[0]) + sh_ref[0]).astype(o_ref.dtype)


def _norm_mod(x, gain, mod3, shift_idx, scale_idx, *, ts=256):
    bsz, s, d = x.shape
    return pl.pallas_call(
        _normmod_kernel,
        grid=(bsz, s // ts),
        in_specs=[
            pl.BlockSpec((1, ts, d), lambda b, i: (b, i, 0)),
            pl.BlockSpec((1, d), lambda b, i: (0, 0)),
            pl.BlockSpec((1, 1, d), lambda b, i: (b * N_MOD + scale_idx, 0, 0)),
            pl.BlockSpec((1, 1, d), lambda b, i: (b * N_MOD + shift_idx, 0, 0)),
        ],
        out_specs=pl.BlockSpec((1, ts, d), lambda b, i: (b, i, 0)),
        out_shape=jax.ShapeDtypeStruct((bsz, s, d), BF16),
        compiler_params=_cparams(("parallel", "parallel")),
        name="norm_mod",
    )(x, gain, mod3, mod3)


def _proj_heads_kernel(a_ref, w_ref, gain_ref, o_ref, *, n_norm_tiles, heads_per_tile):
    j = pl.program_id(2)
    k = a_ref.shape[-1]
    acc = _dot_ref_chunks(lambda a, b: a_ref[0, :, a:b], lambda a, b: w_ref[0, a:b, :], k)

    @pl.when(j < n_norm_tiles)
    def _():
        for hh in range(heads_per_tile):
            blk = acc[:, hh * HEAD_DIM:(hh + 1) * HEAD_DIM]
            ms = jnp.mean(blk * blk, axis=-1, keepdims=True)
            y = blk * lax.rsqrt(ms + EPS) * gain_ref[:, hh * HEAD_DIM:(hh + 1) * HEAD_DIM]
            o_ref[0, hh] = y.astype(o_ref.dtype)

    @pl.when(j >= n_norm_tiles)
    def _():
        for hh in range(heads_per_tile):
            o_ref[0, hh] = acc[:, hh * HEAD_DIM:(hh + 1) * HEAD_DIM].astype(o_ref.dtype)


def _proj_heads(h, w_in, gains, n_cols, n_norm_cols, *, tm=1024, tn=512):
    bsz, s, d = h.shape
    hpt = tn // HEAD_DIM
    kern = functools.partial(_proj_heads_kernel, n_norm_tiles=n_norm_cols // tn, heads_per_tile=hpt)
    return pl.pallas_call(
        kern,
        grid=(bsz, s // tm, n_cols // tn),
        in_specs=[
            pl.BlockSpec((1, tm, d), lambda b, i, j: (b, i, 0)),
            pl.BlockSpec((1, d, tn), lambda b, i, j: (0, 0, j)),
            pl.BlockSpec((1, tn), lambda b, i, j: (0, j)),
        ],
        out_specs=pl.BlockSpec((1, hpt, tm, HEAD_DIM), lambda b, i, j: (b, j, i, 0)),
        out_shape=jax.ShapeDtypeStruct((bsz, n_cols // HEAD_DIM, s, HEAD_DIM), BF16),
        compiler_params=_cparams(("parallel", "parallel", "arbitrary")),
        name="in_proj_fox",
    )(h, w_in, gains)


def _proj_plain_kernel(a_ref, w_ref, o_ref):
    k = a_ref.shape[-1]
    acc = _dot_ref_chunks(lambda a, b: a_ref[:, a:b], lambda a, b: w_ref[a:b, :], k)
    o_ref[...] = acc.astype(o_ref.dtype)


def _proj_plain(a, w, out_dtype, *, tm=1024, tn=512, name="in_proj"):
    m, k = a.shape
    n = w.shape[1]
    tn = min(tn, n)
    return pl.pallas_call(
        _proj_plain_kernel,
        grid=(m // tm, n // tn),
        in_specs=[
            pl.BlockSpec((tm, k), lambda i, j: (i, 0)),
            pl.BlockSpec((k, tn), lambda i, j: (0, j)),
        ],
        out_specs=pl.BlockSpec((tm, tn), lambda i, j: (i, j)),
        out_shape=jax.ShapeDtypeStruct((m, n), out_dtype),
        compiler_params=_cparams(("parallel", "arbitrary")),
        name=name,
    )(a, w)


def _cum_kernel(f_ref, bias_ref, o_ref, *, n_heads, n_blk):
    rows = n_heads * n_blk
    x = f_ref[0].reshape(rows, HEAD_DIM) + bias_ref[...]
    ls = jnp.minimum(x, 0.0) - jnp.log1p(jnp.exp(-jnp.abs(x)))
    r = lax.broadcasted_iota(jnp.int32, (HEAD_DIM, HEAD_DIM), 0)
    c = lax.broadcasted_iota(jnp.int32, (HEAD_DIM, HEAD_DIM), 1)
    upper = _ones_where(r <= c)
    within = _dot_ones_rhs(ls, upper)
    tot = jnp.broadcast_to(within[:, HEAD_DIM - 1:HEAD_DIM], (rows, HEAD_DIM))
    rr = lax.broadcasted_iota(jnp.int32, (rows, rows), 0)
    cc = lax.broadcasted_iota(jnp.int32, (rows, rows), 1)
    prev_same_head = _ones_where((_shr(rr, n_blk) == _shr(cc, n_blk)) & (cc < rr))
    off = _dot_ones_lhs(prev_same_head, tot)
    o_ref[0] = ((within + off) * LOG2E).reshape(n_heads, n_blk, HEAD_DIM)


def _fox_cum(f_t, bias_rows):
    bsz, n_heads, s = f_t.shape
    n_blk = s // HEAD_DIM
    f4 = f_t.reshape(bsz, n_heads, n_blk, HEAD_DIM)
    kern = functools.partial(_cum_kernel, n_heads=n_heads, n_blk=n_blk)
    out = pl.pallas_call(
        kern,
        grid=(bsz,),
        in_specs=[
            pl.BlockSpec((1, n_heads, n_blk, HEAD_DIM), lambda b: (b, 0, 0, 0)),
            pl.BlockSpec((n_heads * n_blk, 1), lambda b: (0, 0)),
        ],
        out_specs=pl.BlockSpec((1, n_heads, n_blk, HEAD_DIM), lambda b: (b, 0, 0, 0)),
        out_shape=jax.ShapeDtypeStruct((bsz, n_heads, n_blk, HEAD_DIM), F32),
        compiler_params=_cparams(("parallel",)),
        name="fox_cum",
    )(f4, bias_rows)
    return out.reshape(bsz, n_heads, s)


def _fox_kernel(q_ref, k_ref, v_ref, ck_ref, o_ref, *, tq):
    qi = pl.program_id(2)
    q = q_ref[0, 0]

    def step(j, carry, masked):
        m, l, acc = carry
        start = pl.multiple_of(j * tq, tq)
        kj = k_ref[0, 0, pl.ds(start, tq), :]
        vj = v_ref[0, 0, pl.ds(start, tq), :]
        s = _dot_nt(q, kj) - ck_ref[0, 0, j]
        if masked:
            row = lax.broadcasted_iota(jnp.int32, (tq, tq), 0)
            col = lax.broadcasted_iota(jnp.int32, (tq, tq), 1)
            s = jnp.where(col <= row, s, -jnp.inf)
        m_new = jnp.maximum(m, jnp.max(s, axis=-1, keepdims=True))
        alpha = jnp.exp2(m - m_new)
        p = jnp.exp2(s - m_new)
        l = alpha * l + jnp.sum(p, axis=-1, keepdims=True)
        acc = alpha * acc + _dot(p.astype(BF16), vj)
        return m_new, l, acc

    init = (jnp.full((tq, 1), -jnp.inf, F32), jnp.zeros((tq, 1), F32), jnp.zeros((tq, HEAD_DIM), F32))
    carry = lax.fori_loop(0, qi, lambda j, c: step(j, c, False), init)
    _, l, acc = step(qi, carry, True)
    o_ref[0] = (acc / l).astype(o_ref.dtype)


def _fox_attention(qkv, ck, *, tq=512):
    bsz, h3, s, _ = qkv.shape
    n_heads = h3 // 3
    nq = s // tq
    ck5 = ck.reshape(bsz, n_heads, nq, 1, tq)
    return pl.pallas_call(
        functools.partial(_fox_kernel, tq=tq),
        grid=(bsz, n_heads, nq),
        in_specs=[
            pl.BlockSpec((1, 1, tq, HEAD_DIM), lambda b, h, i: (b, h, i, 0)),
            pl.BlockSpec((1, 1, s, HEAD_DIM), lambda b, h, i: (b, n_heads + h, 0, 0)),
            pl.BlockSpec((1, 1, s, HEAD_DIM), lambda b, h, i: (b, 2 * n_heads + h, 0, 0)),
            pl.BlockSpec((1, 1, nq, 1, tq), lambda b, h, i: (b, h, 0, 0, 0)),
        ],
        out_specs=pl.BlockSpec((1, tq, HEAD_DIM), lambda b, h, i: (b, i, h)),
        out_shape=jax.ShapeDtypeStruct((bsz, s, n_heads * HEAD_DIM), BF16),
        compiler_params=_cparams(("parallel", "parallel", "arbitrary")),
        name="fox_attn",
    )(qkv, qkv, qkv, ck5)


def _unit_lower_inverse(a_strict, same16, same32, eye):
    c = a_strict.shape[0]
    d1 = jnp.where(same16, a_strict, 0.0)
    e1 = jnp.where(same32 & jnp.logical_not(same16), a_strict, 0.0).astype(BF16)
    e2 = jnp.where(same32, 0.0, a_strict).astype(BF16)
    d1b = d1.astype(BF16)
    d2 = _dot(d1b, d1b)
    p = eye - d1
    r = _dot(d2.astype(BF16), jnp.concatenate([d2, p], axis=1).astype(BF16))
    d4 = r[:, :c]
    p = p + r[:, c:]
    r = _dot(d4.astype(BF16), jnp.concatenate([d4, p], axis=1).astype(BF16))
    d8 = r[:, :c]
    p = p + r[:, c:]
    p = p + _dot(d8.astype(BF16), p.astype(BF16))
    pb = p.astype(BF16)
    p = p - _dot(pb, _dot(e1, pb).astype(BF16))
    pb = p.astype(BF16)
    p = p - _dot(pb, _dot(e2, pb).astype(BF16))
    return p


def _gdn_kernel(p_ref, z_ref, a_ref, b_ref, at_ref, cw_ref, alog_r_ref, dtb_r_ref, alog_c_ref, dtb_c_ref,
                nw_ref, o_ref, state_ref, xbuf_ref, *, n_heads):
    c = CHUNK
    width = n_heads * HEAD_DIM

    @pl.when(pl.program_id(1) == 0)
    def _():
        state_ref[...] = jnp.zeros_like(state_ref)
        xbuf_ref[0:SUBLANES, :] = jnp.zeros((SUBLANES, 3 * width), F32)

    xbuf_ref[SUBLANES:SUBLANES + c, :] = p_ref[0]

    g_col = -jnp.exp(alog_r_ref[...]) * _softplus(a_ref[0] + dtb_r_ref[...])
    beta = jax.nn.sigmoid(b_ref[0])
    g_row = -jnp.exp(alog_c_ref[...]) * _softplus(at_ref[0, 0] + dtb_c_ref[...])
    ri = lax.broadcasted_iota(jnp.int32, (c, c), 0)
    ci = lax.broadcasted_iota(jnp.int32, (c, c), 1)
    lower = ri >= ci
    strict = ri > ci
    same16 = _shr(ri, 16) == _shr(ci, 16)
    same32 = _shr(ri, 32) == _shr(ci, 32)
    eye = (ri == ci).astype(F32)
    gc_col = _dot_ones_lhs(_ones_where(lower), g_col)
    gc_row = _dot_ones_rhs(g_row, _ones_where(ri <= ci))
    e_g = jnp.exp(gc_col)
    e_tail = jnp.exp(gc_col[c - 1:c, :] - gc_col)
    g_tot = jnp.exp(gc_col[c - 1:c, :])

    def conv_silu(col0):
        cols = slice(col0, col0 + HEAD_DIM)
        y = xbuf_ref[SUBLANES - 3:SUBLANES - 3 + c, cols] * cw_ref[0:1, cols]
        for i in range(1, CONV_K):
            lo = SUBLANES - 3 + i
            y = y + xbuf_ref[lo:lo + c, cols] * cw_ref[i:i + 1, cols]
        return _silu(y)

    def l2norm(x):
        return x * lax.rsqrt(jnp.sum(x * x, axis=-1, keepdims=True) + EPS)

    for h in range(n_heads):
        q = l2norm(conv_silu(h * HEAD_DIM)) * (HEAD_DIM ** -0.5)
        k = l2norm(conv_silu(width + h * HEAD_DIM))
        v = conv_silu(2 * width + h * HEAD_DIM)
        beta_h = beta[:, h:h + 1]
        eg_h = e_g[:, h:h + 1]
        decay = jnp.exp(jnp.where(lower, gc_col[:, h:h + 1] - gc_row[h:h + 1, :], -jnp.inf))
        kb = k * beta_h
        kq = _dot_nt(jnp.concatenate([kb, q], axis=0).astype(BF16), k.astype(BF16))
        a_strict = jnp.where(strict, kq[:c] * decay, 0.0)
        intra = kq[c:] * decay
        t_inv = _unit_lower_inverse(a_strict, same16, same32, eye)
        rhs = jnp.concatenate([v * beta_h, kb * eg_h], axis=1).astype(BF16)
        uw = _dot(t_inv.astype(BF16), rhs)
        u = uw[:, :HEAD_DIM]
        w = uw[:, HEAD_DIM:]
        state = state_ref[h]
        ws_qs = _dot(jnp.concatenate([w, q * eg_h], axis=0).astype(BF16), state.astype(BF16))
        v_new = u - ws_qs[:c]
        o = ws_qs[c:] + _dot(intra.astype(BF16), v_new.astype(BF16))
        k_tail = (k * e_tail[:, h:h + 1]).astype(BF16)
        state_ref[h] = state * g_tot[:, h:h + 1] + _dot_tn(k_tail, v_new.astype(BF16))
        cols = slice(h * HEAD_DIM, (h + 1) * HEAD_DIM)
        o_n = o * lax.rsqrt(jnp.mean(o * o, axis=-1, keepdims=True) + EPS) * nw_ref[...]
        o_ref[0, :, cols] = (o_n * _silu(z_ref[0, :, cols].astype(F32))).astype(o_ref.dtype)

    xbuf_ref[0:SUBLANES, :] = xbuf_ref[c:c + SUBLANES, :]


def _gdn(p_qkv, z, a, b, conv_w, a_log, dt_bias, norm_w):
    bsz, s, w3 = p_qkv.shape
    n_heads = a.shape[-1]
    width = n_heads * HEAD_DIM
    n_chunks = s // CHUNK
    a_t = a.reshape(bsz, n_chunks, CHUNK, n_heads).transpose(0, 1, 3, 2)
    row = lambda t: t.reshape(1, n_heads)
    col = lambda t: t.reshape(n_heads, 1)
    full = lambda shape: pl.BlockSpec(shape, lambda bb, cc: (0,) * len(shape))
    return pl.pallas_call(
        functools.partial(_gdn_kernel, n_heads=n_heads),
        grid=(bsz, n_chunks),
        in_specs=[
            pl.BlockSpec((1, CHUNK, w3), lambda bb, cc: (bb, cc, 0)),
            pl.BlockSpec((1, CHUNK, width), lambda bb, cc: (bb, cc, 0)),
            pl.BlockSpec((1, CHUNK, n_heads), lambda bb, cc: (bb, cc, 0)),
            pl.BlockSpec((1, CHUNK, n_heads), lambda bb, cc: (bb, cc, 0)),
            pl.BlockSpec((1, 1, n_heads, CHUNK), lambda bb, cc: (bb, cc, 0, 0)),
            full((CONV_K, w3)),
            full((1, n_heads)), full((1, n_heads)), full((n_heads, 1)), full((n_heads, 1)),
            full((1, HEAD_DIM)),
        ],
        out_specs=pl.BlockSpec((1, CHUNK, width), lambda bb, cc: (bb, cc, 0)),
        out_shape=jax.ShapeDtypeStruct((bsz, s, width), BF16),
        scratch_shapes=[
            pltpu.VMEM((n_heads, HEAD_DIM, HEAD_DIM), F32),
            pltpu.VMEM((SUBLANES + CHUNK, w3), F32),
        ],
        compiler_params=_cparams(("parallel", "arbitrary")),
        name="gdn",
    )(p_qkv, z, a, b, a_t, conv_w, row(a_log), row(dt_bias), col(a_log), col(dt_bias), norm_w.reshape(1, HEAD_DIM))


def _outproj_kernel(y1_ref, y2_ref, w_ref, x_ref, g_ref, o_ref):
    k1 = y1_ref.shape[-1]
    k2 = y2_ref.shape[-1]
    acc = (_dot_ref_chunks(lambda a, b: y1_ref[0, :, a:b], lambda a, b: w_ref[0, a:b, :], k1)
           + _dot_ref_chunks(lambda a, b: y2_ref[0, :, a:b], lambda a, b: w_ref[0, k1 + a:k1 + b, :], k2))
    o_ref[0] = x_ref[0] + g_ref[0] * acc


def _out_proj(y1, y2, w_out, x, mod3, gate_idx, *, tm=1024, tn=512):
    bsz, s, k1 = y1.shape
    k2 = y2.shape[-1]
    d = x.shape[-1]
    return pl.pallas_call(
        _outproj_kernel,
        grid=(bsz, s // tm, d // tn),
        in_specs=[
            pl.BlockSpec((1, tm, k1), lambda b, i, j: (b, i, 0)),
            pl.BlockSpec((1, tm, k2), lambda b, i, j: (b, i, 0)),
            pl.BlockSpec((1, k1 + k2, tn), lambda b, i, j: (0, 0, j)),
            pl.BlockSpec((1, tm, tn), lambda b, i, j: (b, i, j)),
            pl.BlockSpec((1, 1, tn), lambda b, i, j: (b * N_MOD + gate_idx, 0, j)),
        ],
        out_specs=pl.BlockSpec((1, tm, tn), lambda b, i, j: (b, i, j)),
        out_shape=jax.ShapeDtypeStruct((bsz, s, d), F32),
        compiler_params=_cparams(("parallel", "parallel", "arbitrary")),
        name="out_proj",
    )(y1, y2, w_out, x, mod3)


def _ffn_up_kernel(a_ref, wg_ref, wu_ref, o_ref):
    k = a_ref.shape[-1]
    g = _dot_ref_chunks(lambda a, b: a_ref[:, a:b], lambda a, b: wg_ref[0, a:b, :], k)
    u = _dot_ref_chunks(lambda a, b: a_ref[:, a:b], lambda a, b: wu_ref[0, a:b, :], k)
    o_ref[...] = (_silu(g) * u).astype(o_ref.dtype)


def _ffn_up(a, w_gate, w_up, *, tm=1024, tn=256):
    m, k = a.shape
    n = w_gate.shape[-1]
    return pl.pallas_call(
        _ffn_up_kernel,
        grid=(m // tm, n // tn),
        in_specs=[
            pl.BlockSpec((tm, k), lambda i, j: (i, 0)),
            pl.BlockSpec((1, k, tn), lambda i, j: (0, 0, j)),
            pl.BlockSpec((1, k, tn), lambda i, j: (0, 0, j)),
        ],
        out_specs=pl.BlockSpec((tm, tn), lambda i, j: (i, j)),
        out_shape=jax.ShapeDtypeStruct((m, n), BF16),
        compiler_params=_cparams(("parallel", "arbitrary")),
        name="ffn_up",
    )(a, w_gate, w_up)


def _ffn_down_kernel(a_ref, w_ref, x_ref, g_ref, o_ref):
    k = a_ref.shape[-1]
    acc = _dot_ref_chunks(lambda a, b: a_ref[0, :, a:b], lambda a, b: w_ref[0, a:b, :], k)
    o_ref[0] = x_ref[0] + g_ref[0] * acc


def _ffn_down(act, w_down, x, mod3, gate_idx, *, tm=1024, tn=256):
    bsz, s, f = act.shape
    d = x.shape[-1]
    return pl.pallas_call(
        _ffn_down_kernel,
        grid=(bsz, s // tm, d // tn),
        in_specs=[
            pl.BlockSpec((1, tm, f), lambda b, i, j: (b, i, 0), pipeline_mode=pl.Buffered(1)),
            pl.BlockSpec((1, f, tn), lambda b, i, j: (0, 0, j)),
            pl.BlockSpec((1, tm, tn), lambda b, i, j: (b, i, j)),
            pl.BlockSpec((1, 1, tn), lambda b, i, j: (b * N_MOD + gate_idx, 0, j)),
        ],
        out_specs=pl.BlockSpec((1, tm, tn), lambda b, i, j: (b, i, j)),
        out_shape=jax.ShapeDtypeStruct((bsz, s, d), F32),
        compiler_params=_cparams(("parallel", "parallel", "arbitrary")),
        name="ffn_down",
    )(act, w_down, x, mod3)


def _pick(n, pref):
    t = min(pref, n)
    while n % t:
        t //= 2
    return t


def kernel(x, c, w_ada, b_ada, norm1_g, w_in, fox_q_norm, fox_k_norm, fox_f_bias, gdn_conv_w, gdn_a_log,
           gdn_dt_bias, gdn_norm_w, w_out, norm2_g, w_gate, w_up, w_down):
    bsz, s, d = x.shape
    depth = w_ada.shape[0]
    n_fox = fox_f_bias.shape[-1]
    n_gdn = gdn_a_log.shape[-1]
    fox_w = n_fox * HEAD_DIM
    gdn_w = n_gdn * HEAD_DIM
    off_f = 3 * fox_w
    off_gqkv = off_f + n_fox
    off_a = off_gqkv + 3 * gdn_w
    off_b = off_a + n_gdn
    off_z = off_b + n_gdn
    t = bsz * s
    tm = _pick(s, 1024)

    for l in range(depth):
        mod = _ada_mod(c, w_ada[l:l + 1], b_ada[l:l + 1], tn=_pick(N_MOD * d, 512))
        mod3 = mod.reshape(bsz * N_MOD, 1, d)

        h = _norm_mod(x, norm1_g[l:l + 1], mod3, 0, 1, ts=_pick(s, 256))

        q_gain = jnp.tile(fox_q_norm[l], n_fox) * (LOG2E * HEAD_DIM ** -0.5)
        gains = jnp.concatenate([q_gain, jnp.tile(fox_k_norm[l], n_fox), jnp.ones((fox_w,), F32)])[None]
        tn_f = _pick(fox_w, 512)
        qkv = _proj_heads(h, w_in[l:l + 1], gains, 3 * fox_w, 2 * fox_w, tm=tm, tn=tn_f)

        w_l = w_in[l]
        h2d = h.reshape(t, d)
        small = jnp.concatenate([w_l[:, off_f:off_gqkv], w_l[:, off_a:off_z]], axis=1)
        small = jnp.pad(small, ((0, 0), (0, HEAD_DIM - small.shape[1])))
        p_small = _proj_plain(h2d, small, F32, tm=tm, name="in_proj_gates")
        p_gqkv = _proj_plain(h2d, w_l[:, off_gqkv:off_a], F32, tm=tm, tn=_pick(3 * gdn_w, 512), name="in_proj_gdn")
        p_z = _proj_plain(h2d, w_l[:, off_z:off_z + gdn_w], BF16, tm=tm, tn=_pick(gdn_w, 512), name="in_proj_z")

        f_t = p_small[:, :n_fox].reshape(bsz, s, n_fox).transpose(0, 2, 1)
        bias_rows = jnp.repeat(fox_f_bias[l], s // HEAD_DIM)[:, None]
        ck = _fox_cum(f_t, bias_rows)
        y_fox = _fox_attention(qkv, ck, tq=_pick(s, 512))

        a_in = p_small[:, n_fox:n_fox + n_gdn].reshape(bsz, s, n_gdn)
        b_in = p_small[:, n_fox + n_gdn:n_fox + 2 * n_gdn].reshape(bsz, s, n_gdn)
        y_gdn = _gdn(p_gqkv.reshape(bsz, s, 3 * gdn_w), p_z.reshape(bsz, s, gdn_w), a_in, b_in,
                     gdn_conv_w[l], gdn_a_log[l], gdn_dt_bias[l], gdn_norm_w[l])

        x = _out_proj(y_fox, y_gdn, w_out[l:l + 1], x, mod3, 2, tm=tm, tn=_pick(d, 512))

        h = _norm_mod(x, norm2_g[l:l + 1], mod3, 3, 4, ts=_pick(s, 256))
        act = _ffn_up(h.reshape(t, d), w_gate[l:l + 1], w_up[l:l + 1], tm=tm, tn=_pick(w_gate.shape[-1], 256))
        x = _ffn_down(act.reshape(bsz, s, -1), w_down[l:l + 1], x, mod3, 5, tm=tm, tn=_pick(d, 256))
    return x
```

```python
import functools

import jax
import jax.numpy as jnp
from jax import lax
from jax.experimental import pallas as pl
from jax.experimental.pallas import tpu as pltpu

F32 = jnp.float32
BF16 = jnp.bfloat16

HEAD_DIM = 128
CHUNK = 64
CONV_K = 4
N_MOD = 6
EPS = 1e-6
LOG2E = 1.4426950408889634
V7X_VMEM_LIMIT_BYTES = 56 * 1024 * 1024
SUBLANES = 8
BF16_ROWS = 2 * SUBLANES


def _cparams(sem):
    return pltpu.CompilerParams(dimension_semantics=sem, vmem_limit_bytes=V7X_VMEM_LIMIT_BYTES)


def _silu(x):
    return x * jax.nn.sigmoid(x)


def _dot(a, b):
    return jnp.dot(a, b, preferred_element_type=F32)


def _dot_nt(a, b):
    return lax.dot_general(a, b, (((1,), (1,)), ((), ())), preferred_element_type=F32)


def _dot_tn(a, b):
    return lax.dot_general(a, b, (((0,), (0,)), ((), ())), preferred_element_type=F32)


def _dot_ref_chunks(a_at, w_at, k, tk=512, nt=False):
    acc = None
    for k0 in range(0, k, tk):
        k1 = min(k0 + tk, k)
        part = _dot_nt(a_at(k0, k1), w_at(k0, k1)) if nt else _dot(a_at(k0, k1), w_at(k0, k1))
        acc = part if acc is None else acc + part
    return acc


def _shr(x, n):
    shift = n.bit_length() - 1
    assert 1 << shift == n
    return lax.shift_right_logical(x, shift)


def _softplus(x):
    return jnp.maximum(x, 0.0) + jnp.log1p(jnp.exp(-jnp.abs(x)))


def _ones_where(cond):
    return jnp.where(cond, 1.0, 0.0).astype(BF16)


def _split3(x):
    hi = x.astype(BF16)
    r = x - hi.astype(F32)
    mid = r.astype(BF16)
    lo = (r - mid.astype(F32)).astype(BF16)
    return hi, mid, lo


def _dot_ones_rhs(x, ones_rhs):
    hi, mid, lo = _split3(x)
    return _dot(hi, ones_rhs) + _dot(mid, ones_rhs) + _dot(lo, ones_rhs)


def _dot_ones_lhs(ones_lhs, x):
    hi, mid, lo = _split3(x)
    return _dot(ones_lhs, hi) + _dot(ones_lhs, mid) + _dot(ones_lhs, lo)


def _ada_kernel(c_ref, w_ref, b_ref, o_ref):
    cond = _silu(c_ref[...]).astype(BF16)
    k = c_ref.shape[-1]
    acc = _dot_ref_chunks(lambda a, b: cond[:, a:b], lambda a, b: w_ref[0, a:b, :].astype(BF16), k)
    o_ref[...] = acc + b_ref[...]


def _ada_mod(c, w_ada, b_ada, *, tn=512):
    bsz, d = c.shape
    n = w_ada.shape[-1]
    c8 = jnp.zeros((SUBLANES, d), F32).at[:bsz].set(c)
    out = pl.pallas_call(
        _ada_kernel,
        grid=(n // tn,),
        in_specs=[
            pl.BlockSpec((SUBLANES, d), lambda j: (0, 0)),
            pl.BlockSpec((1, d, tn), lambda j: (0, 0, j)),
            pl.BlockSpec((1, tn), lambda j: (0, j)),
        ],
        out_specs=pl.BlockSpec((SUBLANES, tn), lambda j: (0, j)),
        out_shape=jax.ShapeDtypeStruct((SUBLANES, n), F32),
        compiler_params=_cparams(("arbitrary",)),
        name="ada_mod",
    )(c8, w_ada, b_ada)
    return out[:bsz]


def _normmod_kernel(x_ref, g_ref, sc_ref, sh_ref, o_ref):
    x = x_ref[0]
    ms = jnp.mean(x * x, axis=-1, keepdims=True)
    y = x * lax.rsqrt(ms + EPS) * g_ref[...]
    o_ref[0] = (y * (1.0 + sc_ref[0]) + sh_ref[0]).astype(o_ref.dtype)


def _norm_mod(x, gain, mod3, shift_idx, scale_idx, *, ts=256):
    bsz, s, d = x.shape
    return pl.pallas_call(
        _normmod_kernel,
        grid=(bsz, s // ts),
        in_specs=[
            pl.BlockSpec((1, ts, d), lambda b, i: (b, i, 0)),
            pl.BlockSpec((1, d), lambda b, i: (0, 0)),
            pl.BlockSpec((1, 1, d), lambda b, i: (b * N_MOD + scale_idx, 0, 0)),
            pl.BlockSpec((1, 1, d), lambda b, i: (b * N_MOD + shift_idx, 0, 0)),
        ],
        out_specs=pl.BlockSpec((1, ts, d), lambda b, i: (b, i, 0)),
        out_shape=jax.ShapeDtypeStruct((bsz, s, d), BF16),
        compiler_params=_cparams(("parallel", "parallel")),
        name="norm_mod",
    )(x, gain, mod3, mod3)


def _wt_rows(main_ref, tail_ref, shift, k0, k1):
    if shift == 0:
        return main_ref[0, :, k0:k1].astype(BF16)
    return jnp.concatenate([main_ref[0, shift:, k0:k1], tail_ref[0, :shift, k0:k1]], axis=0).astype(BF16)


def _proj_heads_kernel(a_ref, w_ref, gain_ref, o_ref, *, n_norm_tiles, heads_per_tile):
    j = pl.program_id(2)
    k = a_ref.shape[-1]
    acc = _dot_ref_chunks(lambda a, b: a_ref[0, :, a:b], lambda a, b: _wt_rows(w_ref, None, 0, a, b), k, nt=True)

    @pl.when(j < n_norm_tiles)
    def _():
        for hh in range(heads_per_tile):
            blk = acc[:, hh * HEAD_DIM:(hh + 1) * HEAD_DIM]
            ms = jnp.mean(blk * blk, axis=-1, keepdims=True)
            y = blk * lax.rsqrt(ms + EPS) * gain_ref[:, hh * HEAD_DIM:(hh + 1) * HEAD_DIM]
            o_ref[0, hh] = y.astype(o_ref.dtype)

    @pl.when(j >= n_norm_tiles)
    def _():
        for hh in range(heads_per_tile):
            o_ref[0, hh] = acc[:, hh * HEAD_DIM:(hh + 1) * HEAD_DIM].astype(o_ref.dtype)


def _proj_heads(h, wt, gains, n_rows, n_norm_rows, *, tm, tn):
    bsz, s, d = h.shape
    hpt = tn // HEAD_DIM
    kern = functools.partial(_proj_heads_kernel, n_norm_tiles=n_norm_rows // tn, heads_per_tile=hpt)
    return pl.pallas_call(
        kern,
        grid=(bsz, s // tm, n_rows // tn),
        in_specs=[
            pl.BlockSpec((1, tm, d), lambda b, i, j: (b, i, 0)),
            pl.BlockSpec((1, tn, d), lambda b, i, j: (0, j, 0)),
            pl.BlockSpec((1, tn), lambda b, i, j: (0, j)),
        ],
        out_specs=pl.BlockSpec((1, hpt, tm, HEAD_DIM), lambda b, i, j: (b, j, i, 0)),
        out_shape=jax.ShapeDtypeStruct((bsz, n_rows // HEAD_DIM, s, HEAD_DIM), BF16),
        compiler_params=_cparams(("parallel", "parallel", "arbitrary")),
        name="in_proj_fox",
    )(h, wt, gains)


def _proj_rows_kernel(a_ref, w_ref, t_ref, o_ref, *, shift):
    k = a_ref.shape[-1]
    acc = _dot_ref_chunks(lambda a, b: a_ref[0, :, a:b], lambda a, b: _wt_rows(w_ref, t_ref, shift, a, b), k, nt=True)
    o_ref[0] = acc.astype(o_ref.dtype)


def _proj_rows(h, wt, row0, n_rows, out_dtype, *, tm, tn, name):
    bsz, s, d = h.shape
    base = (row0 // tn) * tn
    shift = row0 - base
    tail = max(BF16_ROWS, 1 << (shift - 1).bit_length()) if shift else BF16_ROWS
    assert shift % BF16_ROWS == 0 and tn % tail == 0 and n_rows % tn == 0
    tail_idx = (lambda j: (base + (j + 1) * tn) // tail) if shift else (lambda j: 0)
    return pl.pallas_call(
        functools.partial(_proj_rows_kernel, shift=shift),
        grid=(bsz, s // tm, n_rows // tn),
        in_specs=[
            pl.BlockSpec((1, tm, d), lambda b, i, j: (b, i, 0)),
            pl.BlockSpec((1, tn, d), lambda b, i, j: (0, base // tn + j, 0)),
            pl.BlockSpec((1, tail, d), lambda b, i, j: (0, tail_idx(j), 0)),
        ],
        out_specs=pl.BlockSpec((1, tm, tn), lambda b, i, j: (b, i, j)),
        out_shape=jax.ShapeDtypeStruct((bsz, s, n_rows), out_dtype),
        compiler_params=_cparams(("parallel", "parallel", "arbitrary")),
        name=name,
    )(h, wt, wt)


def _proj_gates_kernel(a_ref, wf_ref, wa_ref, wb_ref, o_ref):
    k = a_ref.shape[-1]

    def wt_at(k0, k1):
        return jnp.concatenate([wf_ref[0, :, k0:k1], wa_ref[0, :, k0:k1], wb_ref[0, :, k0:k1]], axis=0).astype(BF16)

    o_ref[0] = _dot_ref_chunks(lambda a, b: a_ref[0, :, a:b], wt_at, k, nt=True)


def _proj_gates(h, wt, off_f, off_a, off_b, n_heads, *, tm):
    bsz, s, d = h.shape
    assert n_heads % BF16_ROWS == 0 and off_f % n_heads == 0 and off_a % n_heads == 0 and off_b % n_heads == 0
    spec = lambda off: pl.BlockSpec((1, n_heads, d), lambda b, i: (0, off // n_heads, 0))
    return pl.pallas_call(
        _proj_gates_kernel,
        grid=(bsz, s // tm),
        in_specs=[pl.BlockSpec((1, tm, d), lambda b, i: (b, i, 0)), spec(off_f), spec(off_a), spec(off_b)],
        out_specs=pl.BlockSpec((1, tm, 3 * n_heads), lambda b, i: (b, i, 0)),
        out_shape=jax.ShapeDtypeStruct((bsz, s, 3 * n_heads), F32),
        compiler_params=_cparams(("parallel", "parallel")),
        name="in_proj_gates",
    )(h, wt, wt, wt)


def _cum_kernel(f_ref, bias_ref, o_ref, *, n_heads, n_blk):
    rows = n_heads * n_blk
    x = f_ref[0].reshape(rows, HEAD_DIM) + bias_ref[...]
    ls = jnp.minimum(x, 0.0) - jnp.log1p(jnp.exp(-jnp.abs(x)))
    r = lax.broadcasted_iota(jnp.int32, (HEAD_DIM, HEAD_DIM), 0)
    c = lax.broadcasted_iota(jnp.int32, (HEAD_DIM, HEAD_DIM), 1)
    upper = _ones_where(r <= c)
    within = _dot_ones_rhs(ls, upper)
    tot = jnp.broadcast_to(within[:, HEAD_DIM - 1:HEAD_DIM], (rows, HEAD_DIM))
    rr = lax.broadcasted_iota(jnp.int32, (rows, rows), 0)
    cc = lax.broadcasted_iota(jnp.int32, (rows, rows), 1)
    prev_same_head = _ones_where((_shr(rr, n_blk) == _shr(cc, n_blk)) & (cc < rr))
    off = _dot_ones_lhs(prev_same_head, tot)
    o_ref[0] = ((within + off) * LOG2E).reshape(n_heads, n_blk, HEAD_DIM)


def _fox_cum(f_t, bias_rows):
    bsz, n_heads, s = f_t.shape
    n_blk = s // HEAD_DIM
    f4 = f_t.reshape(bsz, n_heads, n_blk, HEAD_DIM)
    kern = functools.partial(_cum_kernel, n_heads=n_heads, n_blk=n_blk)
    out = pl.pallas_call(
        kern,
        grid=(bsz,),
        in_specs=[
            pl.BlockSpec((1, n_heads, n_blk, HEAD_DIM), lambda b: (b, 0, 0, 0)),
            pl.BlockSpec((n_heads * n_blk, 1), lambda b: (0, 0)),
        ],
        out_specs=pl.BlockSpec((1, n_heads, n_blk, HEAD_DIM), lambda b: (b, 0, 0, 0)),
        out_shape=jax.ShapeDtypeStruct((bsz, n_heads, n_blk, HEAD_DIM), F32),
        compiler_params=_cparams(("parallel",)),
        name="fox_cum",
    )(f4, bias_rows)
    return out.reshape(bsz, n_heads, s)


def _fox_kernel(q_ref, k_ref, v_ref, ck_ref, o_ref, *, tq, hp):
    qi = pl.program_id(2)
    heads = range(hp)
    q = [q_ref[0, h] for h in heads]

    def step(j, carry, masked):
        start = pl.multiple_of(j * tq, tq)
        s = [_dot_nt(q[h], k_ref[0, h, pl.ds(start, tq), :]) - ck_ref[0, h, j] for h in heads]
        if masked:
            row = lax.broadcasted_iota(jnp.int32, (tq, tq), 0)
            col = lax.broadcasted_iota(jnp.int32, (tq, tq), 1)
            s = [jnp.where(col <= row, sh, -jnp.inf) for sh in s]
        out = []
        for h in heads:
            m, l, acc = carry[3 * h:3 * h + 3]
            m_new = jnp.maximum(m, jnp.max(s[h], axis=-1, keepdims=True))
            alpha = jnp.exp2(m - m_new)
            p = jnp.exp2(s[h] - m_new)
            l = alpha * l + jnp.sum(p, axis=-1, keepdims=True)
            acc = alpha * acc + _dot(p.astype(BF16), v_ref[0, h, pl.ds(start, tq), :])
            out += [m_new, l, acc]
        return tuple(out)

    init = (jnp.full((tq, 1), -jnp.inf, F32), jnp.zeros((tq, 1), F32), jnp.zeros((tq, HEAD_DIM), F32)) * hp
    carry = lax.fori_loop(0, qi, lambda j, c: step(j, c, False), init)
    final = step(qi, carry, True)
    for h in heads:
        _, l, acc = final[3 * h:3 * h + 3]
        o_ref[0, :, h * HEAD_DIM:(h + 1) * HEAD_DIM] = (acc / l).astype(o_ref.dtype)


def _fox_attention(qkv, ck, *, tq=512, hp=4):
    bsz, h3, s, _ = qkv.shape
    n_heads = h3 // 3
    ng = n_heads // hp
    nq = s // tq
    ck5 = ck.reshape(bsz, n_heads, nq, 1, tq)
    return pl.pallas_call(
        functools.partial(_fox_kernel, tq=tq, hp=hp),
        grid=(bsz, ng, nq),
        in_specs=[
            pl.BlockSpec((1, hp, tq, HEAD_DIM), lambda b, g, i: (b, g, i, 0)),
            pl.BlockSpec((1, hp, s, HEAD_DIM), lambda b, g, i: (b, ng + g, 0, 0)),
            pl.BlockSpec((1, hp, s, HEAD_DIM), lambda b, g, i: (b, 2 * ng + g, 0, 0)),
            pl.BlockSpec((1, hp, nq, 1, tq), lambda b, g, i: (b, g, 0, 0, 0)),
        ],
        out_specs=pl.BlockSpec((1, tq, hp * HEAD_DIM), lambda b, g, i: (b, i, g)),
        out_shape=jax.ShapeDtypeStruct((bsz, s, n_heads * HEAD_DIM), BF16),
        compiler_params=_cparams(("parallel", "parallel", "arbitrary")),
        name="fox_attn",
    )(qkv, qkv, qkv, ck5)


def _unit_lower_inverse(a_strict, same16, same32, eye):
    n = range(len(a_strict))
    c = a_strict[0].shape[0]
    cat = lambda x, y: jnp.concatenate([x, y], axis=1).astype(BF16)
    d1 = [jnp.where(same16, a, 0.0) for a in a_strict]
    e1 = [jnp.where(same32 & jnp.logical_not(same16), a, 0.0).astype(BF16) for a in a_strict]
    e2 = [jnp.where(same32, 0.0, a).astype(BF16) for a in a_strict]
    d1b = [d.astype(BF16) for d in d1]
    d2 = [_dot(d1b[i], d1b[i]) for i in n]
    p = [eye - d1[i] for i in n]
    r = [_dot(d2[i].astype(BF16), cat(d2[i], p[i])) for i in n]
    d4 = [r[i][:, :c] for i in n]
    p = [p[i] + r[i][:, c:] for i in n]
    r = [_dot(d4[i].astype(BF16), cat(d4[i], p[i])) for i in n]
    d8 = [r[i][:, :c] for i in n]
    p = [p[i] + r[i][:, c:] for i in n]
    p = [p[i] + _dot(d8[i].astype(BF16), p[i].astype(BF16)) for i in n]
    pb = [x.astype(BF16) for x in p]
    f = [_dot(e1[i], pb[i]).astype(BF16) for i in n]
    p = [p[i] - _dot(pb[i], f[i]) for i in n]
    pb = [x.astype(BF16) for x in p]
    f = [_dot(e2[i], pb[i]).astype(BF16) for i in n]
    return [p[i] - _dot(pb[i], f[i]) for i in n]


def _gdn_kernel(p_ref, z_ref, a_ref, b_ref, at_ref, cw_ref, alog_r_ref, dtb_r_ref, alog_c_ref, dtb_c_ref,
                nw_ref, o_ref, state_ref, xbuf_ref, *, n_heads):
    c = CHUNK
    width = n_heads * HEAD_DIM
    heads = range(n_heads)

    @pl.when(pl.program_id(1) == 0)
    def _():
        state_ref[...] = jnp.zeros_like(state_ref)
        xbuf_ref[0:SUBLANES, :] = jnp.zeros((SUBLANES, 3 * width), F32)

    xbuf_ref[SUBLANES:SUBLANES + c, :] = p_ref[0]

    g_col = -jnp.exp(alog_r_ref[...]) * _softplus(a_ref[0] + dtb_r_ref[...])
    beta = jax.nn.sigmoid(b_ref[0])
    g_row = -jnp.exp(alog_c_ref[...]) * _softplus(at_ref[0, 0] + dtb_c_ref[...])
    ri = lax.broadcasted_iota(jnp.int32, (c, c), 0)
    ci = lax.broadcasted_iota(jnp.int32, (c, c), 1)
    lower = ri >= ci
    strict = ri > ci
    same16 = _shr(ri, 16) == _shr(ci, 16)
    same32 = _shr(ri, 32) == _shr(ci, 32)
    eye = (ri == ci).astype(F32)
    gc_col = _dot_ones_lhs(_ones_where(lower), g_col)
    gc_row = _dot_ones_rhs(g_row, _ones_where(ri <= ci))
    e_g = jnp.exp(gc_col)
    e_tail = jnp.exp(gc_col[c - 1:c, :] - gc_col)
    g_tot = jnp.exp(gc_col[c - 1:c, :])

    def conv_silu(col0):
        cols = slice(col0, col0 + HEAD_DIM)
        y = xbuf_ref[SUBLANES - 3:SUBLANES - 3 + c, cols] * cw_ref[0:1, cols]
        for i in range(1, CONV_K):
            lo = SUBLANES - 3 + i
            y = y + xbuf_ref[lo:lo + c, cols] * cw_ref[i:i + 1, cols]
        return _silu(y)

    def l2norm(x):
        return x * lax.rsqrt(jnp.sum(x * x, axis=-1, keepdims=True) + EPS)

    col = lambda x, h: x[:, h:h + 1]
    q = [l2norm(conv_silu(h * HEAD_DIM)) * (HEAD_DIM ** -0.5) for h in heads]
    k = [l2norm(conv_silu(width + h * HEAD_DIM)) for h in heads]
    v = [conv_silu(2 * width + h * HEAD_DIM) for h in heads]
    kb = [k[h] * col(beta, h) for h in heads]
    decay = [jnp.exp(jnp.where(lower, col(gc_col, h) - gc_row[h:h + 1, :], -jnp.inf)) for h in heads]
    kq = [_dot_nt(jnp.concatenate([kb[h], q[h]], axis=0).astype(BF16), k[h].astype(BF16)) for h in heads]
    a_strict = [jnp.where(strict, kq[h][:c] * decay[h], 0.0) for h in heads]
    intra = [(kq[h][c:] * decay[h]).astype(BF16) for h in heads]
    t_inv = _unit_lower_inverse(a_strict, same16, same32, eye)
    rhs = [jnp.concatenate([v[h] * col(beta, h), kb[h] * col(e_g, h)], axis=1).astype(BF16) for h in heads]
    uw = [_dot(t_inv[h].astype(BF16), rhs[h]) for h in heads]
    state = [state_ref[h] for h in heads]
    wq = [jnp.concatenate([uw[h][:, HEAD_DIM:], q[h] * col(e_g, h)], axis=0).astype(BF16) for h in heads]
    ws_qs = [_dot(wq[h], state[h].astype(BF16)) for h in heads]
    v_new = [(uw[h][:, :HEAD_DIM] - ws_qs[h][:c]).astype(BF16) for h in heads]
    o = [ws_qs[h][c:] + _dot(intra[h], v_new[h]) for h in heads]
    k_tail = [(k[h] * col(e_tail, h)).astype(BF16) for h in heads]
    for h in heads:
        state_ref[h] = state[h] * g_tot[:, h:h + 1] + _dot_tn(k_tail[h], v_new[h])
    for h in heads:
        cols = slice(h * HEAD_DIM, (h + 1) * HEAD_DIM)
        o_n = o[h] * lax.rsqrt(jnp.mean(o[h] * o[h], axis=-1, keepdims=True) + EPS) * nw_ref[...]
        o_ref[0, :, cols] = (o_n * _silu(z_ref[0, :, cols].astype(F32))).astype(o_ref.dtype)

    xbuf_ref[0:SUBLANES, :] = xbuf_ref[c:c + SUBLANES, :]


def _gdn(p_qkv, z, gates, conv_w, a_log, dt_bias, norm_w):
    bsz, s, w3 = p_qkv.shape
    n_heads = a_log.shape[-1]
    width = n_heads * HEAD_DIM
    n_chunks = s // CHUNK
    a = gates[..., n_heads:2 * n_heads]
    b = gates[..., 2 * n_heads:]
    a_t = a.reshape(bsz, n_chunks, CHUNK, n_heads).transpose(0, 1, 3, 2)
    row = lambda t: t.reshape(1, n_heads)
    col = lambda t: t.reshape(n_heads, 1)
    full = lambda shape: pl.BlockSpec(shape, lambda bb, cc: (0,) * len(shape))
    return pl.pallas_call(
        functools.partial(_gdn_kernel, n_heads=n_heads),
        grid=(bsz, n_chunks),
        in_specs=[
            pl.BlockSpec((1, CHUNK, w3), lambda bb, cc: (bb, cc, 0)),
            pl.BlockSpec((1, CHUNK, width), lambda bb, cc: (bb, cc, 0)),
            pl.BlockSpec((1, CHUNK, n_heads), lambda bb, cc: (bb, cc, 0)),
            pl.BlockSpec((1, CHUNK, n_heads), lambda bb, cc: (bb, cc, 0)),
            pl.BlockSpec((1, 1, n_heads, CHUNK), lambda bb, cc: (bb, cc, 0, 0)),
            full((CONV_K, w3)),
            full((1, n_heads)), full((1, n_heads)), full((n_heads, 1)), full((n_heads, 1)),
            full((1, HEAD_DIM)),
        ],
        out_specs=pl.BlockSpec((1, CHUNK, width), lambda bb, cc: (bb, cc, 0)),
        out_shape=jax.ShapeDtypeStruct((bsz, s, width), BF16),
        scratch_shapes=[
            pltpu.VMEM((n_heads, HEAD_DIM, HEAD_DIM), F32),
            pltpu.VMEM((SUBLANES + CHUNK, w3), F32),
        ],
        compiler_params=_cparams(("parallel", "arbitrary")),
        name="gdn",
    )(p_qkv, z, a, b, a_t, conv_w, row(a_log), row(dt_bias), col(a_log), col(dt_bias), norm_w.reshape(1, HEAD_DIM))


def _outproj_kernel(y1_ref, y2_ref, w_ref, x_ref, g_ref, o_ref):
    k1 = y1_ref.shape[-1]
    k2 = y2_ref.shape[-1]
    acc = (_dot_ref_chunks(lambda a, b: y1_ref[0, :, a:b], lambda a, b: w_ref[0, a:b, :].astype(BF16), k1)
           + _dot_ref_chunks(lambda a, b: y2_ref[0, :, a:b], lambda a, b: w_ref[0, k1 + a:k1 + b, :].astype(BF16), k2))
    o_ref[0] = x_ref[0] + g_ref[0] * acc


def _out_proj(y1, y2, w_out, x, mod3, gate_idx, *, tm=1024, tn=512):
    bsz, s, k1 = y1.shape
    k2 = y2.shape[-1]
    d = x.shape[-1]
    return pl.pallas_call(
        _outproj_kernel,
        grid=(bsz, s // tm, d // tn),
        in_specs=[
            pl.BlockSpec((1, tm, k1), lambda b, i, j: (b, i, 0)),
            pl.BlockSpec((1, tm, k2), lambda b, i, j: (b, i, 0)),
            pl.BlockSpec((1, k1 + k2, tn), lambda b, i, j: (0, 0, j)),
            pl.BlockSpec((1, tm, tn), lambda b, i, j: (b, i, j)),
            pl.BlockSpec((1, 1, tn), lambda b, i, j: (b * N_MOD + gate_idx, 0, j)),
        ],
        out_specs=pl.BlockSpec((1, tm, tn), lambda b, i, j: (b, i, j)),
        out_shape=jax.ShapeDtypeStruct((bsz, s, d), F32),
        compiler_params=_cparams(("parallel", "parallel", "arbitrary")),
        name="out_proj",
    )(y1, y2, w_out, x, mod3)


def _ffn_up_kernel(a_ref, wg_ref, wu_ref, o_ref):
    k = a_ref.shape[-1]
    g = _dot_ref_chunks(lambda a, b: a_ref[:, a:b], lambda a, b: wg_ref[0, a:b, :].astype(BF16), k)
    u = _dot_ref_chunks(lambda a, b: a_ref[:, a:b], lambda a, b: wu_ref[0, a:b, :].astype(BF16), k)
    o_ref[...] = (_silu(g) * u).astype(o_ref.dtype)


def _ffn_up(a, w_gate, w_up, *, tm=1024, tn=256):
    m, k = a.shape
    n = w_gate.shape[-1]
    return pl.pallas_call(
        _ffn_up_kernel,
        grid=(m // tm, n // tn),
        in_specs=[
            pl.BlockSpec((tm, k), lambda i, j: (i, 0)),
            pl.BlockSpec((1, k, tn), lambda i, j: (0, 0, j)),
            pl.BlockSpec((1, k, tn), lambda i, j: (0, 0, j)),
        ],
        out_specs=pl.BlockSpec((tm, tn), lambda i, j: (i, j)),
        out_shape=jax.ShapeDtypeStruct((m, n), BF16),
        compiler_params=_cparams(("parallel", "arbitrary")),
        name="ffn_up",
    )(a, w_gate, w_up)


def _ffn_down_kernel(a_ref, w_ref, x_ref, g_ref, o_ref):
    k = a_ref.shape[-1]
    acc = _dot_ref_chunks(lambda a, b: a_ref[0, :, a:b], lambda a, b: w_ref[0, a:b, :].astype(BF16), k)
    o_ref[0] = x_ref[0] + g_ref[0] * acc


def _ffn_down(act, w_down, x, mod3, gate_idx, *, tm=1024, tn=256):
    bsz, s, f = act.shape
    d = x.shape[-1]
    return pl.pallas_call(
        _ffn_down_kernel,
        grid=(bsz, s // tm, d // tn),
        in_specs=[
            pl.BlockSpec((1, tm, f), lambda b, i, j: (b, i, 0), pipeline_mode=pl.Buffered(1)),
            pl.BlockSpec((1, f, tn), lambda b, i, j: (0, 0, j)),
            pl.BlockSpec((1, tm, tn), lambda b, i, j: (b, i, j)),
            pl.BlockSpec((1, 1, tn), lambda b, i, j: (b * N_MOD + gate_idx, 0, j)),
        ],
        out_specs=pl.BlockSpec((1, tm, tn), lambda b, i, j: (b, i, j)),
        out_shape=jax.ShapeDtypeStruct((bsz, s, d), F32),
        compiler_params=_cparams(("parallel", "parallel", "arbitrary")),
        name="ffn_down",
    )(act, w_down, x, mod3)


def _pick(n, pref):
    t = min(pref, n)
    while n % t:
        t //= 2
    return t


def kernel(x, c, w_ada, b_ada, norm1_g, w_in, fox_q_norm, fox_k_norm, fox_f_bias, gdn_conv_w, gdn_a_log,
           gdn_dt_bias, gdn_norm_w, w_out, norm2_g, w_gate, w_up, w_down):
    bsz, s, d = x.shape
    depth = w_ada.shape[0]
    n_heads = fox_f_bias.shape[-1]
    assert gdn_a_log.shape[-1] == n_heads
    width = n_heads * HEAD_DIM
    off_f = 3 * width
    off_gqkv = off_f + n_heads
    off_a = off_gqkv + 3 * width
    off_b = off_a + n_heads
    off_z = off_b + n_heads
    t = bsz * s
    tm = _pick(s, 1024)
    tn = _pick(width, 512)
    wt_all = jnp.swapaxes(w_in, 1, 2)

    for l in range(depth):
        mod = _ada_mod(c, w_ada[l:l + 1], b_ada[l:l + 1], tn=_pick(N_MOD * d, 512))
        mod3 = mod.reshape(bsz * N_MOD, 1, d)
        wt = wt_all[l:l + 1]

        h = _norm_mod(x, norm1_g[l:l + 1], mod3, 0, 1, ts=_pick(s, 256))

        q_gain = jnp.tile(fox_q_norm[l], n_heads) * (LOG2E * HEAD_DIM ** -0.5)
        gains = jnp.concatenate([q_gain, jnp.tile(fox_k_norm[l], n_heads), jnp.ones((width,), F32)])[None]
        qkv = _proj_heads(h, wt, gains, 3 * width, 2 * width, tm=tm, tn=tn)
        gates = _proj_gates(h, wt, off_f, off_a, off_b, n_heads, tm=tm)
        p_gqkv = _proj_rows(h, wt, off_gqkv, 3 * width, F32, tm=tm, tn=tn, name="in_proj_gdn")
        p_z = _proj_rows(h, wt, off_z, width, BF16, tm=tm, tn=tn, name="in_proj_z")

        f_t = gates[..., :n_heads].transpose(0, 2, 1)
        bias_rows = jnp.repeat(fox_f_bias[l], s // HEAD_DIM)[:, None]
        ck = _fox_cum(f_t, bias_rows)
        y_fox = _fox_attention(qkv, ck, tq=_pick(s, 512), hp=_pick(n_heads, 4))

        y_gdn = _gdn(p_gqkv, p_z, gates, gdn_conv_w[l], gdn_a_log[l], gdn_dt_bias[l], gdn_norm_w[l])

        x = _out_proj(y_fox, y_gdn, w_out[l:l + 1], x, mod3, 2, tm=tm, tn=_pick(d, 512))

        h = _norm_mod(x, norm2_g[l:l + 1], mod3, 3, 4, ts=_pick(s, 256))
        act = _ffn_up(h.reshape(t, d), w_gate[l:l + 1], w_up[l:l + 1], tm=tm, tn=_pick(w_gate.shape[-1], 256))
        x = _ffn_down(act.reshape(bsz, s, -1), w_down[l:l + 1], x, mod3, 5, tm=tm, tn=_pick(d, 256))
    return x
```
